```python
import math
import jax, jax.numpy as jnp
from jax import lax
import numpy as np

D_MODEL = 2048
BATCH = 4
SEQ = 2048
DEPTH = 4

CHUNK = 64
Q_BLOCK = 128
MIX_W = D_MODEL // 2
N_BRANCH = 3
MLA_HEADS = 8
MLA_NOPE = 128
MLA_ROPE = 64
MLA_QK = MLA_NOPE + MLA_ROPE
MLA_V = MIX_W // MLA_HEADS
Q_LORA = D_MODEL // 4
KV_LORA = D_MODEL // 8
ROPE_THETA = 10000.0
FOX_DH = 128
FOX_HEADS = MIX_W // FOX_DH
CH_DH = 128
CH_HEADS = MIX_W // CH_DH
LEFT_CHUNKS = 8
BAND = (LEFT_CHUNKS + 1) * CHUNK
REL_CLIP = 128
N_REL = 2 * REL_CLIP + 1
MEM_LEN = 256
X_HEADS = 4
X_DH = 128
D_FF = 4 * D_MODEL
EPS = 1e-6
NEG = -1e30

SPLIT_SIZES = (Q_LORA, KV_LORA, MLA_ROPE, 3 * FOX_HEADS * FOX_DH, FOX_HEADS,
               3 * CH_HEADS * CH_DH, N_BRANCH * D_MODEL)
D_IN = sum(SPLIT_SIZES)
SPLIT_CUTS = tuple(int(c) for c in np.cumsum(SPLIT_SIZES)[:-1])

kernel_name = 'hybrid_mla_fox_chunkrel_gated_encoder'


def rms_norm(x, g):
    xf = x.astype(jnp.float32)
    y = xf * lax.rsqrt(jnp.mean(xf * xf, axis=-1, keepdims=True) + EPS)
    return (y * g.astype(jnp.float32)).astype(x.dtype)


def rope_tables(seq):
    pos = jnp.arange(seq, dtype=jnp.float32)
    inv = ROPE_THETA ** (-jnp.arange(0, MLA_ROPE, 2, dtype=jnp.float32) / MLA_ROPE)
    ang = pos[:, None] * inv[None, :]
    return jnp.cos(ang), jnp.sin(ang)


def apply_rope(x, cos, sin):
    x1, x2 = jnp.split(x, 2, axis=-1)
    c = cos.astype(x.dtype)[None, :, None, :]
    s = sin.astype(x.dtype)[None, :, None, :]
    return jnp.concatenate([x1 * c - x2 * s, x1 * s + x2 * c], axis=-1)


def block_sweep_attention(q, k, v, cum=None):
    B, S, H, Dk = q.shape
    scale = Dk ** -0.5
    nq = S // Q_BLOCK
    qb = jnp.moveaxis(q.reshape(B, nq, Q_BLOCK, H, Dk), 1, 0)
    k_pos = jnp.arange(S)
    if cum is None:
        xs = (jnp.arange(nq), qb)
    else:
        cb = jnp.moveaxis(cum.reshape(B, nq, Q_BLOCK, H), 1, 0)
        cum_k = jnp.transpose(cum, (0, 2, 1))[:, :, None, :]
        xs = (jnp.arange(nq), qb, cb)

    def body(xs_):
        i, q_i = xs_[0], xs_[1]
        q_pos = i * Q_BLOCK + jnp.arange(Q_BLOCK)
        s = jnp.einsum('bqhd,bkhd->bhqk', q_i, k).astype(jnp.float32) * scale
        if cum is None:
            allowed = (k_pos // CHUNK)[None, :] <= (q_pos // CHUNK)[:, None]
        else:
            allowed = k_pos[None, :] <= q_pos[:, None]
            s = s + jnp.transpose(xs_[2], (0, 2, 1))[..., None] - cum_k
        s = jnp.where(allowed[None, None], s, NEG)
        p = jax.nn.softmax(s, axis=-1).astype(v.dtype)
        return jnp.einsum('bhqk,bkhd->bqhd', p, v)

    out = lax.map(body, xs)
    return jnp.moveaxis(out, 0, 1).reshape(B, S, H, v.shape[-1])


def mla_branch(c_q_raw, c_kv_raw, k_r_raw, g_cq, w_uq, g_ckv, w_ukv, g_qn, g_kn, cos, sin):
    B, S, _ = c_q_raw.shape
    q = (rms_norm(c_q_raw, g_cq) @ w_uq).reshape(B, S, MLA_HEADS, MLA_QK)
    kv = (rms_norm(c_kv_raw, g_ckv) @ w_ukv).reshape(B, S, MLA_HEADS, MLA_NOPE + MLA_V)
    k_nope, v = kv[..., :MLA_NOPE], kv[..., MLA_NOPE:]
    k_rope = jnp.broadcast_to(k_r_raw[:, :, None, :], (B, S, MLA_HEADS, MLA_ROPE))
    k = jnp.concatenate([k_nope, k_rope], axis=-1)
    q = rms_norm(q, g_qn)
    k = rms_norm(k, g_kn)
    q = jnp.concatenate([q[..., :MLA_NOPE], apply_rope(q[..., MLA_NOPE:], cos, sin)], axis=-1)
    k = jnp.concatenate([k[..., :MLA_NOPE], apply_rope(k[..., MLA_NOPE:], cos, sin)], axis=-1)
    o = block_sweep_attention(q, k, v)
    return o.reshape(B, S, MIX_W)


def fox_branch(qkv, f_logit, b_f, g_qn, g_kn):
    B, S, _ = qkv.shape
    qkv = qkv.reshape(B, S, 3, FOX_HEADS, FOX_DH)
    q = rms_norm(qkv[:, :, 0], g_qn)
    k = rms_norm(qkv[:, :, 1], g_kn)
    v = qkv[:, :, 2]
    log_f = jax.nn.log_sigmoid(f_logit.astype(jnp.float32) + b_f.astype(jnp.float32))
    cum = jnp.cumsum(log_f, axis=1)
    o = block_sweep_attention(q, k, v, cum)
    return o.reshape(B, S, MIX_W)


def chunk_band_branch(qkv, rel_bias, g_qn, g_kn):
    B, S, _ = qkv.shape
    n_chunks = S // CHUNK
    pad = LEFT_CHUNKS * CHUNK
    qkv = qkv.reshape(B, S, 3, CH_HEADS, CH_DH)
    q = rms_norm(qkv[:, :, 0], g_qn)
    k = rms_norm(qkv[:, :, 1], g_kn)
    v = qkv[:, :, 2]
    band_idx = (jnp.arange(n_chunks) * CHUNK)[:, None] + jnp.arange(BAND)[None, :]
    kp = jnp.pad(k, ((0, 0), (pad, 0), (0, 0), (0, 0)))
    vp = jnp.pad(v, ((0, 0), (pad, 0), (0, 0), (0, 0)))
    kb = kp[:, band_idx]
    vb = vp[:, band_idx]
    qc = q.reshape(B, n_chunks, CHUNK, CH_HEADS, CH_DH)
    s = jnp.einsum('bcqhd,bckhd->bchqk', qc, kb).astype(jnp.float32) * (CH_DH ** -0.5)
    rel = (jnp.arange(CHUNK)[:, None] + pad) - jnp.arange(BAND)[None, :]
    bias = rel_bias.astype(jnp.float32)[:, jnp.clip(rel, -REL_CLIP, REL_CLIP) + REL_CLIP]
    valid = band_idx >= pad
    s = jnp.where(valid[None, :, None, None, :], s + bias[None, None], NEG)
    p = jax.nn.softmax(s, axis=-1).astype(v.dtype)
    o = jnp.einsum('bchqk,bckhd->bcqhd', p, vb)
    return o.reshape(B, S, MIX_W)


def memory_cross_attention(h, mem_n, w_xq, w_xkv, g_qn, g_kn, w_xo):
    B, S, _ = h.shape
    M = mem_n.shape[1]
    q = rms_norm((h @ w_xq).reshape(B, S, X_HEADS, X_DH), g_qn)
    kv = (mem_n @ w_xkv).reshape(B, M, 2, X_HEADS, X_DH)
    k = rms_norm(kv[:, :, 0], g_kn)
    v = kv[:, :, 1]
    s = jnp.einsum('bshd,bmhd->bhsm', q, k).astype(jnp.float32) * (X_DH ** -0.5)
    p = jax.nn.softmax(s, axis=-1).astype(v.dtype)
    o = jnp.einsum('bhsm,bmhd->bshd', p, v).reshape(B, S, X_HEADS * X_DH)
    return o @ w_xo


def setup_inputs(seed: int = 0) -> dict:
    key = jax.random.key(seed)
    ks = jax.random.split(key, 32)
    f32 = jnp.float32

    def nrm(k, shape, scale):
        return jax.random.normal(k, shape, f32) * scale

    def gain(k, shape):
        return 1.0 + 0.02 * jax.random.normal(k, shape, f32)

    L = DEPTH
    return {
        'x': nrm(ks[0], (BATCH, SEQ, D_MODEL), 1.0),
        'mem': nrm(ks[1], (BATCH, MEM_LEN, D_MODEL), 1.0),
        'g_mix': gain(ks[2], (L, D_MODEL)),
        'w_in': nrm(ks[3], (L, D_MODEL, D_IN), D_MODEL ** -0.5),
        'g_cq': gain(ks[4], (L, Q_LORA)),
        'w_uq': nrm(ks[5], (L, Q_LORA, MLA_HEADS * MLA_QK), Q_LORA ** -0.5),
        'g_ckv': gain(ks[6], (L, KV_LORA)),
        'w_ukv': nrm(ks[7], (L, KV_LORA, MLA_HEADS * (MLA_NOPE + MLA_V)), KV_LORA ** -0.5),
        'g_mla_q': gain(ks[8], (L, MLA_QK)),
        'g_mla_k': gain(ks[9], (L, MLA_QK)),
        'b_f': 3.0 + 0.1 * jax.random.normal(ks[10], (L, FOX_HEADS), f32),
        'g_fox_q': gain(ks[11], (L, FOX_DH)),
        'g_fox_k': gain(ks[12], (L, FOX_DH)),
        'rel_bias': nrm(ks[13], (L, CH_HEADS, N_REL), 0.5),
        'g_ch_q': gain(ks[14], (L, CH_DH)),
        'g_ch_k': gain(ks[15], (L, CH_DH)),
        'w_br': nrm(ks[16], (L, N_BRANCH, MIX_W, D_MODEL), MIX_W ** -0.5),
        'w_out': nrm(ks[17], (L, D_MODEL, D_MODEL), D_MODEL ** -0.5),
        'g_cross': gain(ks[18], (L, D_MODEL)),
        'g_mem': gain(ks[19], (L, D_MODEL)),
        'w_xq': nrm(ks[20], (L, D_MODEL, X_HEADS * X_DH), D_MODEL ** -0.5),
        'w_xkv': nrm(ks[21], (L, D_MODEL, 2 * X_HEADS * X_DH), D_MODEL ** -0.5),
        'g_x_q': gain(ks[22], (L, X_DH)),
        'g_x_k': gain(ks[23], (L, X_DH)),
        'w_xo': nrm(ks[24], (L, X_HEADS * X_DH, D_MODEL), (X_HEADS * X_DH) ** -0.5),
        'g_mlp': gain(ks[25], (L, D_MODEL)),
        'w_1': nrm(ks[26], (L, D_MODEL, D_FF), D_MODEL ** -0.5),
        'w_2': nrm(ks[27], (L, D_FF, D_MODEL), D_FF ** -0.5),
    }


def reference(x, mem, g_mix, w_in, g_cq, w_uq, g_ckv, w_ukv, g_mla_q, g_mla_k, b_f,
              g_fox_q, g_fox_k, rel_bias, g_ch_q, g_ch_k, w_br, w_out, g_cross, g_mem,
              w_xq, w_xkv, g_x_q, g_x_k, w_xo, g_mlp, w_1, w_2):
    B, S, _ = x.shape
    cos, sin = rope_tables(S)
    for l in range(DEPTH):
        h = rms_norm(x, g_mix[l])
        z = h @ w_in[l]
        c_q, c_kv, k_r, fox_qkv, fox_f, ch_qkv, gate_logits = jnp.split(z, SPLIT_CUTS, axis=-1)
        y_a = mla_branch(c_q, c_kv, k_r, g_cq[l], w_uq[l], g_ckv[l], w_ukv[l],
                         g_mla_q[l], g_mla_k[l], cos, sin)
        y_b = fox_branch(fox_qkv, fox_f, b_f[l], g_fox_q[l], g_fox_k[l])
        y_c = chunk_band_branch(ch_qkv, rel_bias[l], g_ch_q[l], g_ch_k[l])
        ys = jnp.stack([y_a, y_b, y_c], axis=2)
        proj = jnp.einsum('bsnc,ncd->bsnd', ys, w_br[l])
        gates = jax.nn.sigmoid(gate_logits.astype(jnp.float32)).astype(x.dtype)
        gates = gates.reshape(B, S, N_BRANCH, D_MODEL)
        merged = jnp.einsum('bsnd,bsnd->bsd', gates, proj)
        x = x + merged @ w_out[l]
        x = x + memory_cross_attention(rms_norm(x, g_cross[l]), rms_norm(mem, g_mem[l]),
                                       w_xq[l], w_xkv[l], g_x_q[l], g_x_k[l], w_xo[l])
        hm = rms_norm(x, g_mlp[l])
        x = x + jnp.square(jax.nn.relu(hm @ w_1[l])) @ w_2[l]
    return x
```

```python
import functools

import jax
import jax.numpy as jnp
from jax import lax
from jax.experimental import pallas as pl
from jax.experimental.pallas import tpu as pltpu

F32 = jnp.float32
BF16 = jnp.bfloat16

D_MODEL = 2048
BATCH = 4
SEQ = 2048
DEPTH = 4
TOKENS = BATCH * SEQ

CHUNK = 64
MIX_W = D_MODEL // 2
N_BRANCH = 3
HEADS = 8
DH = 128
MLA_NOPE = 128
MLA_ROPE = 64
MLA_QK = MLA_NOPE + MLA_ROPE
MLA_QK_PAD = 256
Q_LORA = D_MODEL // 4
KV_LORA = D_MODEL // 8
ROPE_THETA = 10000.0
LEFT_CHUNKS = 8
REL_CLIP = 128
MEM_LEN = 256
X_HEADS = 4
D_FF = 4 * D_MODEL
EPS = 1e-6
NEG = -1e30

LANE = 128
V7X_VMEM_BYTES = 64 * 1024 * 1024

_CUT_CQ = 0
_CUT_CKV = Q_LORA
_CUT_KR = _CUT_CKV + KV_LORA
_CUT_FOX = _CUT_KR + MLA_ROPE
_CUT_FOXF = _CUT_FOX + 3 * MIX_W
_CUT_CH = _CUT_FOXF + HEADS
_CUT_GATE = _CUT_CH + 3 * MIX_W
_D_IN = _CUT_GATE + N_BRANCH * D_MODEL

LAT_W = 1024
MAIN_W = 6 * MIX_W + N_BRANCH * D_MODEL
IN_TN = 1024
IN_STEPS = (LAT_W + MAIN_W) // IN_TN

CHUNK_TQ = 128
CHUNK_WIN = CHUNK_TQ + LEFT_CHUNKS * CHUNK


def _cparams(sem, vmem_bytes=48 * 1024 * 1024):
    assert vmem_bytes < V7X_VMEM_BYTES
    return pltpu.CompilerParams(dimension_semantics=sem, vmem_limit_bytes=vmem_bytes)


def _rms(x, g):
    ms = jnp.mean(x * x, axis=-1, keepdims=True)
    return x * lax.rsqrt(ms + EPS) * g


def _dot(a, b):
    return jnp.dot(a, b, preferred_element_type=F32)


def _dot_nt(a, b):
    return lax.dot_general(a, b, (((1,), (1,)), ((), ())), preferred_element_type=F32)


def _in_proj_kernel(x_ref, g_ref, w_ref, gain_ref, lat_ref, main_ref, xn_ref):
    j = pl.program_id(1)

    @pl.when(j == 0)
    def _():
        xn_ref[...] = _rms(x_ref[...], g_ref[...]).astype(BF16)

    acc = _dot(xn_ref[...], w_ref[...])

    @pl.when(j == 0)
    def _():
        lat_ref[...] = acc

    @pl.when((j == 1) | (j == 2) | (j == 4) | (j == 5))
    def _():
        for h in range(HEADS):
            sl = slice(h * DH, (h + 1) * DH)
            main_ref[:, sl] = _rms(acc[:, sl], gain_ref[:, sl]).astype(BF16)

    @pl.when((j == 3) | (j == 6))
    def _():
        main_ref[...] = acc.astype(BF16)

    @pl.when(j >= 7)
    def _():
        main_ref[...] = jax.nn.sigmoid(acc).astype(BF16)


def _in_proj(x, g_mix, w_in_p, gain_in, l, tm=512):
    return pl.pallas_call(
        _in_proj_kernel,
        grid=(TOKENS // tm, IN_STEPS),
        in_specs=[
            pl.BlockSpec((tm, D_MODEL), lambda i, j: (i, 0)),
            pl.BlockSpec((None, 1, D_MODEL), lambda i, j: (l, 0, 0)),
            pl.BlockSpec((None, D_MODEL, IN_TN), lambda i, j: (l, 0, j)),
            pl.BlockSpec((None, 1, IN_TN), lambda i, j: (l, 0, j)),
        ],
        out_specs=[
            pl.BlockSpec((tm, LAT_W), lambda i, j: (i, 0)),
            pl.BlockSpec((tm, IN_TN), lambda i, j: (i, jnp.maximum(j - 1, 0))),
        ],
        out_shape=[
            jax.ShapeDtypeStruct((TOKENS, LAT_W), F32),
            jax.ShapeDtypeStruct((TOKENS, MAIN_W), BF16),
        ],
        scratch_shapes=[pltpu.VMEM((tm, D_MODEL), BF16)],
        compiler_params=_cparams(("parallel", "arbitrary")),
        name="in_proj",
    )(x, g_mix, w_in_p, gain_in)


def _rope(t, cs, sn):
    return t * cs + pltpu.roll(t, 2 * 32, 1) * sn


def _mla_q_kernel(c_ref, g_ref, w_ref, gq_ref, cos_ref, sin_ref, q_ref):
    cn = _rms(c_ref[...], g_ref[...]).astype(BF16)
    qq = _dot(cn, w_ref[...])
    cs = cos_ref[...]
    sn = sin_ref[...]
    gq = gq_ref[...]
    for h in range(HEADS):
        o = h * MLA_QK_PAD
        nope = qq[:, o:o + MLA_NOPE]
        rp = qq[:, o + MLA_NOPE:o + MLA_QK_PAD]
        ss = jnp.sum(nope * nope, axis=-1, keepdims=True) + jnp.sum(rp * rp, axis=-1, keepdims=True)
        r = lax.rsqrt(ss / MLA_QK + EPS)
        q_ref[:, o:o + MLA_NOPE] = (nope * r * gq[:, :MLA_NOPE]).astype(BF16)
        q_ref[:, o + MLA_NOPE:o + MLA_QK_PAD] = _rope(rp * r * gq[:, MLA_NOPE:], cs, sn).astype(BF16)


def _mla_q(lat, g_cq, w_uq_p, gq, cos_t, sin_t, l, tm=512):
    sb = SEQ // tm
    return pl.pallas_call(
        _mla_q_kernel,
        grid=(TOKENS // tm,),
        in_specs=[
            pl.BlockSpec((tm, Q_LORA), lambda i: (i, 0)),
            pl.BlockSpec((None, 1, Q_LORA), lambda i: (l, 0, 0)),
            pl.BlockSpec((None, Q_LORA, HEADS * MLA_QK_PAD), lambda i: (l, 0, 0)),
            pl.BlockSpec((None, 1, MLA_QK_PAD), lambda i: (l, 0, 0)),
            pl.BlockSpec((tm, LANE), lambda i: (i % sb, 0)),
            pl.BlockSpec((tm, LANE), lambda i: (i % sb, 0)),
        ],
        out_specs=pl.BlockSpec((tm, HEADS * MLA_QK_PAD), lambda i: (i, 0)),
        out_shape=jax.ShapeDtypeStruct((TOKENS, HEADS * MLA_QK_PAD), BF16),
        compiler_params=_cparams(("parallel",)),
        name="mla_q",
    )(lat, g_cq, w_uq_p, gq, cos_t, sin_t)


def _mla_kv_kernel(c_ref, kr_ref, g_ref, w_ref, gk_ref, cos_ref, sin_ref, k_ref, v_ref):
    cn = _rms(c_ref[...], g_ref[...]).astype(BF16)
    kv = _dot(cn, w_ref[...])
    gk = gk_ref[...]
    kr = kr_ref[:, :LANE]
    ss_r = jnp.sum(kr * kr, axis=-1, keepdims=True)
    kr_rot = _rope(kr * gk[:, MLA_NOPE:], cos_ref[...], sin_ref[...])
    for h in range(HEADS):
        kn = kv[:, h * MLA_NOPE:(h + 1) * MLA_NOPE]
        ss = jnp.sum(kn * kn, axis=-1, keepdims=True) + ss_r
        r = lax.rsqrt(ss / MLA_QK + EPS)
        o = h * MLA_QK_PAD
        k_ref[:, o:o + MLA_NOPE] = (kn * r * gk[:, :MLA_NOPE]).astype(BF16)
        k_ref[:, o + MLA_NOPE:o + MLA_QK_PAD] = (kr_rot * r).astype(BF16)
    v_ref[...] = kv[:, HEADS * MLA_NOPE:].astype(BF16)


def _mla_kv(lat, g_ckv, w_ukv_p, gk, cos_t, sin_t, l, tm=512):
    sb = SEQ // tm
    return pl.pallas_call(
        _mla_kv_kernel,
        grid=(TOKENS // tm,),
        in_specs=[
            pl.BlockSpec((tm, KV_LORA), lambda i: (i, Q_LORA // KV_LORA)),
            pl.BlockSpec((tm, 2 * LANE), lambda i: (i, (Q_LORA + KV_LORA) // (2 * LANE))),
            pl.BlockSpec((None, 1, KV_LORA), lambda i: (l, 0, 0)),
            pl.BlockSpec((None, KV_LORA, 2 * MIX_W), lambda i: (l, 0, 0)),
            pl.BlockSpec((None, 1, MLA_QK_PAD), lambda i: (l, 0, 0)),
            pl.BlockSpec((tm, LANE), lambda i: (i % sb, 0)),
            pl.BlockSpec((tm, LANE), lambda i: (i % sb, 0)),
        ],
        out_specs=[
            pl.BlockSpec((tm, HEADS * MLA_QK_PAD), lambda i: (i, 0)),
            pl.BlockSpec((tm, MIX_W), lambda i: (i, 0)),
        ],
        out_shape=[
            jax.ShapeDtypeStruct((TOKENS, HEADS * MLA_QK_PAD), BF16),
            jax.ShapeDtypeStruct((TOKENS, MIX_W), BF16),
        ],
        compiler_params=_cparams(("parallel",)),
        name="mla_kv",
    )(lat, lat, g_ckv, w_ukv_p, gk, cos_t, sin_t)


CUM_BLK = 256


def _cum_kernel(f_ref, b_ref, o_ref):
    row = lax.broadcasted_iota(jnp.int32, (CUM_BLK, CUM_BLK), 0)
    col = lax.broadcasted_iota(jnp.int32, (CUM_BLK, CUM_BLK), 1)
    tri = (col <= row).astype(BF16)
    carry = jnp.zeros((1, LANE), F32)
    for i in range(SEQ // CUM_BLK):
        f = f_ref[i * CUM_BLK:(i + 1) * CUM_BLK, LANE:] + b_ref[...]
        lf = jax.nn.log_sigmoid(f)
        hi = lf.astype(BF16)
        r1 = lf - hi.astype(F32)
        mid = r1.astype(BF16)
        lo = (r1 - mid.astype(F32)).astype(BF16)
        c = _dot(tri, hi) + _dot(tri, mid) + _dot(tri, lo) + carry
        o_ref[i * CUM_BLK:(i + 1) * CUM_BLK, :] = c
        carry = c[CUM_BLK - 1:CUM_BLK, :]


def _cum(lat, b_f_p, l):
    return pl.pallas_call(
        _cum_kernel,
        grid=(BATCH,),
        in_specs=[
            pl.BlockSpec((SEQ, 2 * LANE), lambda b: (b, (Q_LORA + KV_LORA) // (2 * LANE))),
            pl.BlockSpec((None, 1, LANE), lambda b: (l, 0, 0)),
        ],
        out_specs=pl.BlockSpec((SEQ, LANE), lambda b: (b, 0)),
        out_shape=jax.ShapeDtypeStruct((TOKENS, LANE), F32),
        compiler_params=_cparams(("parallel",)),
        name="fox_cum",
    )(lat, b_f_p)


def _flash_kernel(*refs, tq, frame_causal, has_bias):
    if has_bias:
        q_ref, k_ref, v_ref, cq_ref, ck_ref, o_ref = refs
    else:
        q_ref, k_ref, v_ref, o_ref = refs
    tk = tq
    row = lax.broadcasted_iota(jnp.int32, (tq, tk), 0)
    col = lax.broadcasted_iota(jnp.int32, (tq, tk), 1)
    if frame_causal:
        diag_ok = col <= row
    else:
        diag_ok = (col >> 6) <= (row >> 6)
    for qi in range(SEQ // tq):
        qs = slice(qi * tq, (qi + 1) * tq)
        q = q_ref[qs, :]
        m = jnp.full((tq, 1), NEG, F32)
        lsum = jnp.zeros((tq, 1), F32)
        acc = jnp.zeros((tq, DH), F32)
        for ki in range(qi + 1):
            ks = slice(ki * tk, (ki + 1) * tk)
            s = _dot_nt(q, k_ref[ks, :])
            if has_bias:
                s = s + (jnp.tile(cq_ref[qs, :], (1, tk // LANE)) - ck_ref[:, ks])
            if ki == qi:
                s = jnp.where(diag_ok, s, NEG)
            m_new = jnp.maximum(m, jnp.max(s, axis=-1, keepdims=True))
            alpha = jnp.exp(m - m_new)
            p = jnp.exp(s - m_new)
            lsum = alpha * lsum + jnp.sum(p, axis=-1, keepdims=True)
            acc = alpha * acc + _dot(p.astype(BF16), v_ref[ks, :])
            m = m_new
        o_ref[qs, :] = (acc / lsum).astype(BF16)


def _flash(q_arr, q_off, k_arr, k_off, v_arr, v_off, dqk, frame_causal, bias=None, tq=512):
    in_specs = [
        pl.BlockSpec((SEQ, dqk), lambda b, h: (b, q_off + h)),
        pl.BlockSpec((SEQ, dqk), lambda b, h: (b, k_off + h)),
        pl.BlockSpec((SEQ, DH), lambda b, h: (b, v_off + h)),
    ]
    args = [q_arr, k_arr, v_arr]
    if bias is not None:
        cq, ck = bias
        in_specs += [
            pl.BlockSpec((None, None, SEQ, LANE), lambda b, h: (b, h, 0, 0)),
            pl.BlockSpec((None, None, 1, SEQ), lambda b, h: (b, h, 0, 0)),
        ]
        args += [cq, ck]
    return pl.pallas_call(
        functools.partial(_flash_kernel, tq=tq, frame_causal=frame_causal, has_bias=bias is not None),
        grid=(BATCH, HEADS),
        in_specs=in_specs,
        out_specs=pl.BlockSpec((SEQ, DH), lambda b, h: (b, h)),
        out_shape=jax.ShapeDtypeStruct((TOKENS, MIX_W), BF16),
        compiler_params=_cparams(("parallel", "parallel")),
        name="flash_fox" if frame_causal else "flash_mla",
    )(*args)


def _chunk_kernel(q_ref, k_ref, v_ref, bias_ref, o_ref):
    for qi in range(SEQ // CHUNK_TQ):
        t0 = qi * CHUNK_TQ
        win_lo = t0 - LEFT_CHUNKS * CHUNK
        lo = max(0, win_lo)
        q = q_ref[t0:t0 + CHUNK_TQ, :]
        s = _dot_nt(q, k_ref[lo:t0 + CHUNK_TQ, :]) + bias_ref[:, lo - win_lo:]
        m = jnp.max(s, axis=-1, keepdims=True)
        p = jnp.exp(s - m)
        den = jnp.sum(p, axis=-1, keepdims=True)
        o = _dot(p.astype(BF16), v_ref[lo:t0 + CHUNK_TQ, :])
        o_ref[t0:t0 + CHUNK_TQ, :] = (o / den).astype(BF16)


def _chunk_attn(main, bias_tile, l):
    qo, ko, vo = 3 * HEADS, 4 * HEADS, 5 * HEADS
    return pl.pallas_call(
        _chunk_kernel,
        grid=(BATCH, HEADS),
        in_specs=[
            pl.BlockSpec((SEQ, DH), lambda b, h: (b, qo + h)),
            pl.BlockSpec((SEQ, DH), lambda b, h: (b, ko + h)),
            pl.BlockSpec((SEQ, DH), lambda b, h: (b, vo + h)),
            pl.BlockSpec((None, None, CHUNK_TQ, CHUNK_WIN), lambda b, h: (l, h, 0, 0)),
        ],
        out_specs=pl.BlockSpec((SEQ, DH), lambda b, h: (b, h)),
        out_shape=jax.ShapeDtypeStruct((TOKENS, MIX_W), BF16),
        compiler_params=_cparams(("parallel", "parallel")),
        name="chunk_attn",
    )(main, main, main, bias_tile)


def _merge_kernel(ya_ref, yb_ref, yc_ref, w_ref, ga_ref, gb_ref, gc_ref, o_ref):
    acc = ga_ref[...].astype(F32) * _dot(ya_ref[...], w_ref[0])
    acc = acc + gb_ref[...].astype(F32) * _dot(yb_ref[...], w_ref[1])
    acc = acc + gc_ref[...].astype(F32) * _dot(yc_ref[...], w_ref[2])
    o_ref[...] = acc.astype(BF16)


def _merge(ya, yb, yc, w_br, main, l, tm=512, tn=512):
    g0 = 6 * MIX_W // tn
    gstep = D_MODEL // tn
    y_spec = pl.BlockSpec((tm, MIX_W), lambda i, j: (i, 0))
    return pl.pallas_call(
        _merge_kernel,
        grid=(TOKENS // tm, D_MODEL // tn),
        in_specs=[
            y_spec, y_spec, y_spec,
            pl.BlockSpec((None, N_BRANCH, MIX_W, tn), lambda i, j: (l, 0, 0, j)),
            pl.BlockSpec((tm, tn), lambda i, j: (i, g0 + j)),
            pl.BlockSpec((tm, tn), lambda i, j: (i, g0 + gstep + j)),
            pl.BlockSpec((tm, tn), lambda i, j: (i, g0 + 2 * gstep + j)),
        ],
        out_specs=pl.BlockSpec((tm, tn), lambda i, j: (i, j)),
        out_shape=jax.ShapeDtypeStruct((TOKENS, D_MODEL), BF16),
        compiler_params=_cparams(("parallel", "arbitrary")),
        name="merge",
    )(ya, yb, yc, w_br, main, main, main)


def _mm_res_kernel(a_ref, w_ref, r_ref, o_ref, acc_ref):
    k = pl.program_id(2)

    @pl.when(k == 0)
    def _():
        acc_ref[...] = jnp.zeros_like(acc_ref)

    acc_ref[...] += _dot(a_ref[...], w_ref[...])

    @pl.when(k == pl.num_programs(2) - 1)
    def _():
        o_ref[...] = r_ref[...] + acc_ref[...]


def _mm_res(a, w, res, l, tm=512, tn=1024, tk=2048):
    kdim = a.shape[1]
    n = w.shape[2]
    return pl.pallas_call(
        _mm_res_kernel,
        grid=(TOKENS // tm, n // tn, kdim // tk),
        in_specs=[
            pl.BlockSpec((tm, tk), lambda i, j, k: (i, k)),
            pl.BlockSpec((None, tk, tn), lambda i, j, k: (l, k, j)),
            pl.BlockSpec((tm, tn), lambda i, j, k: (i, j)),
        ],
        out_specs=pl.BlockSpec((tm, tn), lambda i, j, k: (i, j)),
        out_shape=jax.ShapeDtypeStruct((TOKENS, n), F32),
        scratch_shapes=[pltpu.VMEM((tm, tn), F32)],
        compiler_params=_cparams(("parallel", "parallel", "arbitrary")),
        name="mm_res",
    )(a, w, res)


def _rms_mm_kernel(x_ref, g_ref, w_ref, gain_ref, o_ref, xn_ref, *, n_norm_heads, relu2):
    j = pl.program_id(1)

    @pl.when(j == 0)
    def _():
        xn_ref[...] = _rms(x_ref[...], g_ref[...]).astype(BF16)

    acc = _dot(xn_ref[...], w_ref[...])
    if relu2:
        o_ref[...] = jnp.square(jnp.maximum(acc, 0.0)).astype(BF16)
    else:
        tn = acc.shape[1]
        for h in range(n_norm_heads):
            sl = slice(h * DH, (h + 1) * DH)
            o_ref[:, sl] = _rms(acc[:, sl], gain_ref[:, sl]).astype(BF16)
        if n_norm_heads * DH < tn:
            o_ref[:, n_norm_heads * DH:] = acc[:, n_norm_heads * DH:].astype(BF16)


def _rms_mm(x, g, w, gain, l, *, tm, tn, n_norm_heads=0, relu2=False):
    rows = x.shape[0]
    n = w.shape[2]
    return pl.pallas_call(
        functools.partial(_rms_mm_kernel, n_norm_heads=n_norm_heads, relu2=relu2),
        grid=(rows // tm, n // tn),
        in_specs=[
            pl.BlockSpec((tm, D_MODEL), lambda i, j: (i, 0)),
            pl.BlockSpec((None, 1, D_MODEL), lambda i, j: (l, 0, 0)),
            pl.BlockSpec((None, D_MODEL, tn), lambda i, j: (l, 0, j)),
            pl.BlockSpec((None, 1, tn), lambda i, j: (l, 0, j)),
        ],
        out_specs=pl.BlockSpec((tm, tn), lambda i, j: (i, j)),
        out_shape=jax.ShapeDtypeStruct((rows, n), BF16),
        scratch_shapes=[pltpu.VMEM((tm, D_MODEL), BF16)],
        compiler_params=_cparams(("parallel", "arbitrary")),
        name="rms_mm_relu2" if relu2 else "rms_mm_norm",
    )(x, g, w, gain)


def _cross_kernel(q_ref, kv_ref, w_ref, x_ref, o_ref):
    xw = X_HEADS * DH
    outs = []
    for h in range(X_HEADS):
        sl = slice(h * DH, (h + 1) * DH)
        s = _dot_nt(q_ref[:, sl], kv_ref[:, sl])
        m = jnp.max(s, axis=-1, keepdims=True)
        p = jnp.exp(s - m)
        den = jnp.sum(p, axis=-1, keepdims=True)
        o = _dot(p.astype(BF16), kv_ref[:, xw + h * DH:xw + (h + 1) * DH])
        outs.append((o / den).astype(BF16))
    o_all = jnp.concatenate(outs, axis=1)
    o_ref[...] = x_ref[...] + _dot(o_all, w_ref[...])


def _cross(qx, kvx, w_xo, x, l, tm=512):
    sb = SEQ // tm
    xw = X_HEADS * DH
    return pl.pallas_call(
        _cross_kernel,
        grid=(BATCH, sb),
        in_specs=[
            pl.BlockSpec((tm, xw), lambda b, i: (b * sb + i, 0)),
            pl.BlockSpec((MEM_LEN, 2 * xw), lambda b, i: (b, 0)),
            pl.BlockSpec((None, xw, D_MODEL), lambda b, i: (l, 0, 0)),
            pl.BlockSpec((tm, D_MODEL), lambda b, i: (b * sb + i, 0)),
        ],
        out_specs=pl.BlockSpec((tm, D_MODEL), lambda b, i: (b * sb + i, 0)),
        out_shape=jax.ShapeDtypeStruct((TOKENS, D_MODEL), F32),
        compiler_params=_cparams(("parallel", "parallel")),
        name="cross_attn",
    )(qx, kvx, w_xo, x)


def _rope_lanes(a, b):
    z = jnp.zeros_like(a)
    return jnp.concatenate([a, z, b, z], axis=-1)


def _prep_weights(w_in, w_uq, w_ukv, g_mla_q, g_mla_k, b_f, g_fox_q, g_fox_k, rel_bias, g_ch_q, g_ch_k,
                  g_x_q, g_x_k):
    L = DEPTH
    half = MLA_ROPE // 2
    zpad = jnp.zeros((L, D_MODEL, LANE - HEADS), F32)
    w_lat = jnp.concatenate([
        w_in[:, :, _CUT_CQ:_CUT_KR],
        _rope_lanes(w_in[:, :, _CUT_KR:_CUT_KR + half], w_in[:, :, _CUT_KR + half:_CUT_FOX]),
        w_in[:, :, _CUT_FOXF:_CUT_CH], zpad], axis=-1)
    w_in_p = jnp.concatenate([
        w_lat, w_in[:, :, _CUT_FOX:_CUT_FOXF], w_in[:, :, _CUT_CH:_CUT_GATE], w_in[:, :, _CUT_GATE:]],
        axis=-1).astype(BF16)

    att_scale = DH ** -0.5
    ones = jnp.ones((L, MIX_W), F32)
    gain_in = jnp.concatenate([
        jnp.ones((L, LAT_W), F32),
        jnp.tile(g_fox_q, (1, HEADS)) * att_scale, jnp.tile(g_fox_k, (1, HEADS)), ones,
        jnp.tile(g_ch_q, (1, HEADS)) * att_scale, jnp.tile(g_ch_k, (1, HEADS)), ones,
        jnp.ones((L, N_BRANCH * D_MODEL), F32)], axis=-1)[:, None, :]

    wq = w_uq.reshape(L, Q_LORA, HEADS, MLA_QK)
    w_uq_p = jnp.concatenate([
        wq[..., :MLA_NOPE],
        _rope_lanes(wq[..., MLA_NOPE:MLA_NOPE + half], wq[..., MLA_NOPE + half:])],
        axis=-1).reshape(L, Q_LORA, HEADS * MLA_QK_PAD).astype(BF16)
    wkv = w_ukv.reshape(L, KV_LORA, HEADS, 2, MLA_NOPE)
    w_ukv_p = jnp.transpose(wkv, (0, 1, 3, 2, 4)).reshape(L, KV_LORA, 2 * MIX_W).astype(BF16)

    def qk_gain(g, scale):
        return (jnp.concatenate([
            g[:, :MLA_NOPE],
            _rope_lanes(g[:, MLA_NOPE:MLA_NOPE + half], g[:, MLA_NOPE + half:])], axis=-1) * scale)[:, None, :]

    gq = qk_gain(g_mla_q, MLA_QK ** -0.5)
    gk = qk_gain(g_mla_k, 1.0)

    b_f_p = jnp.concatenate([b_f, jnp.zeros((L, LANE - HEADS), F32)], axis=-1)[:, None, :]

    i = jnp.arange(CHUNK_TQ)[:, None]
    jj = jnp.arange(CHUNK_WIN)[None, :]
    rel = i - jj + LEFT_CHUNKS * CHUNK
    idx = jnp.clip(rel, -REL_CLIP, REL_CLIP) + REL_CLIP
    qc = i // CHUNK
    kc = jj // CHUNK
    ok = (kc >= qc) & (kc <= qc + LEFT_CHUNKS)
    bias_tile = jnp.where(ok[None, None], rel_bias[:, :, idx], NEG)

    gain_xq = (jnp.tile(g_x_q, (1, X_HEADS)) * att_scale)[:, None, :]
    gain_xkv = jnp.concatenate([jnp.tile(g_x_k, (1, X_HEADS)), jnp.ones((L, X_HEADS * DH), F32)],
                               axis=-1)[:, None, :]
    return w_in_p, gain_in, w_uq_p, w_ukv_p, gq, gk, b_f_p, bias_tile, gain_xq, gain_xkv


def _rope_tables():
    pos = jnp.arange(SEQ, dtype=F32)
    inv = ROPE_THETA ** (-jnp.arange(0, MLA_ROPE, 2, dtype=F32) / MLA_ROPE)
    ang = pos[:, None] * inv[None, :]
    c, s = jnp.cos(ang), jnp.sin(ang)
    return _rope_lanes(c, c), _rope_lanes(-s, s)


def kernel(x, mem, g_mix, w_in, g_cq, w_uq, g_ckv, w_ukv, g_mla_q, g_mla_k, b_f, g_fox_q, g_fox_k, rel_bias,
           g_ch_q, g_ch_k, w_br, w_out, g_cross, g_mem, w_xq, w_xkv, g_x_q, g_x_k, w_xo, g_mlp, w_1, w_2):
    (w_in_p, gain_in, w_uq_p, w_ukv_p, gq, gk, b_f_p, bias_tile, gain_xq, gain_xkv) = _prep_weights(
        w_in, w_uq, w_ukv, g_mla_q, g_mla_k, b_f, g_fox_q, g_fox_k, rel_bias, g_ch_q, g_ch_k, g_x_q, g_x_k)
    cos_t, sin_t = _rope_tables()
    w_br_b, w_out_b = w_br.astype(BF16), w_out.astype(BF16)
    w_xq_b, w_xkv_b, w_xo_b = w_xq.astype(BF16), w_xkv.astype(BF16), w_xo.astype(BF16)
    w_1_b, w_2_b = w_1.astype(BF16), w_2.astype(BF16)
    row = lambda g: g[:, None, :]
    g_mix3, g_cq3, g_ckv3, g_cross3, g_mem3, g_mlp3 = map(row, (g_mix, g_cq, g_ckv, g_cross, g_mem, g_mlp))
    ones_ff = jnp.ones((DEPTH, 1, D_FF), F32)

    xs = x.reshape(TOKENS, D_MODEL)
    mem2 = mem.reshape(BATCH * MEM_LEN, D_MODEL)
    for l in range(DEPTH):
        lat, main = _in_proj(xs, g_mix3, w_in_p, gain_in, l)
        q_a = _mla_q(lat, g_cq3, w_uq_p, gq, cos_t, sin_t, l)
        k_a, v_a = _mla_kv(lat, g_ckv3, w_ukv_p, gk, cos_t, sin_t, l)
        y_a = _flash(q_a, 0, k_a, 0, v_a, 0, MLA_QK_PAD, frame_causal=False)

        cum = _cum(lat, b_f_p, l)[:, :HEADS].reshape(BATCH, SEQ, HEADS)
        cum_t = jnp.transpose(cum, (0, 2, 1))
        cq = jnp.broadcast_to(cum_t[..., None], (BATCH, HEADS, SEQ, LANE))
        ck = cum_t[:, :, None, :]
        y_b = _flash(main, 0, main, HEADS, main, 2 * HEADS, DH, frame_causal=True, bias=(cq, ck))

        y_c = _chunk_attn(main, bias_tile, l)

        merged = _merge(y_a, y_b, y_c, w_br_b, main, l)
        xs = _mm_res(merged, w_out_b, xs, l)

        qx = _rms_mm(xs, g_cross3, w_xq_b, gain_xq, l, tm=512, tn=X_HEADS * DH, n_norm_heads=X_HEADS)
        kvx = _rms_mm(mem2, g_mem3, w_xkv_b, gain_xkv, l, tm=BATCH * MEM_LEN // 2, tn=2 * X_HEADS * DH,
                      n_norm_heads=X_HEADS)
        xs = _cross(qx, kvx, w_xo_b, xs, l)

        a = _rms_mm(xs, g_mlp3, w_1_b, ones_ff, l, tm=512, tn=1024, relu2=True)
        xs = _mm_res(a, w_2_b, xs, l)
    return xs.reshape(BATCH, SEQ, D_MODEL)
```
